```python
import jax
import jax.numpy as jnp
from jax import lax
import numpy as np

D_MODEL = 2048
BATCH = 1
SEQ = 16384
DEPTH = 2

GRID_W = 64
CTX_LEN = 256
NORM_EPS = 1e-6
N_ADA = 6
D_LRU = D_MODEL // 2
LRU_HEADS = D_LRU // 128
LRU_BW = D_LRU // LRU_HEADS
CONV_W = 4
CONV_LEFT = 2
RGLRU_C = 8.0
D_POOL = D_MODEL // 4
POOL_WINDOWS = (2, 4, 8, 16)
N_POOL_GROUPS = len(POOL_WINDOWS)
POOL_GC = D_POOL // N_POOL_GROUPS
POOL_OUT_GC = D_MODEL // N_POOL_GROUPS
D_FOURIER = D_MODEL // 4
FOURIER_GROUPS = 4
FOURIER_GC = D_FOURIER // FOURIER_GROUPS
N_BRANCHES = 3
I_YA = D_LRU
I_XB = 2 * D_LRU
I_XC = I_XB + D_POOL
I_G = I_XC + D_FOURIER
D_IN = I_G + N_BRANCHES * D_MODEL
D_FF = ((8 * D_MODEL + 3 * 256 - 1) // (3 * 256)) * 256

kernel_name = "hybrid_gated_lru_pool_fourier_dit"


def _rms_norm(x, g):
    xf = x.astype(jnp.float32)
    y = xf * lax.rsqrt(jnp.mean(xf * xf, axis=-1, keepdims=True) + NORM_EPS)
    return (y * g.astype(jnp.float32)).astype(x.dtype)


def _modulate(h, shift, scale):
    return h * (1 + scale) + shift


def _centred_depthwise_conv(x, w, b):
    y = lax.conv_general_dilated(
        x, w[:, None, :].astype(x.dtype), window_strides=(1,),
        padding=[(CONV_LEFT, CONV_W - 1 - CONV_LEFT)],
        dimension_numbers=('NWC', 'WIO', 'NWC'), feature_group_count=x.shape[-1])
    return y + b


def _lin_combine(left, right):
    a1, b1 = left
    a2, b2 = right
    return a1 * a2, a2 * b1 + b2


def _rglru_scan(xc, wa, ba, wx, bx, lam, h0, reverse):
    B, L, _ = xc.shape
    f32 = jnp.float32
    xf = xc.astype(f32)
    xh = xf.reshape(B, L, LRU_HEADS, LRU_BW)
    r = jax.nn.sigmoid(jnp.einsum('blhi,hij->blhj', xh, wa.astype(f32)) + ba.astype(f32)).reshape(B, L, D_LRU)
    i = jax.nn.sigmoid(jnp.einsum('blhi,hij->blhj', xh, wx.astype(f32)) + bx.astype(f32)).reshape(B, L, D_LRU)
    log_a = -RGLRU_C * r * jax.nn.softplus(-lam.astype(f32))
    a = jnp.exp(log_a)
    b = jnp.sqrt(-jnp.expm1(2.0 * log_a)) * (i * xf)
    a_cum, h = lax.associative_scan(_lin_combine, (a, b), axis=1, reverse=reverse)
    return h + a_cum * h0.astype(f32)[:, None, :]


def _bidir_rglru(xa, conv_w, conv_b, lru_wa, lru_ba, lru_wx, lru_bx, lru_lambda, h0_f, h0_b):
    xconv = _centred_depthwise_conv(xa, conv_w, conv_b)
    h_f = _rglru_scan(xconv, lru_wa[0], lru_ba[0], lru_wx[0], lru_bx[0], lru_lambda[0], h0_f, False)
    h_b = _rglru_scan(xconv, lru_wa[1], lru_ba[1], lru_wx[1], lru_bx[1], lru_lambda[1], h0_b, True)
    return h_f, h_b


def _multiscale_pool(xb, rows, w_pool, pool_scale):
    B, L, _ = xb.shape
    width = L // rows
    xg = xb.astype(jnp.float32).reshape(B, rows, width, N_POOL_GROUPS, POOL_GC)
    cs = jnp.pad(jnp.cumsum(xg, axis=2), ((0, 0), (0, 0), (1, 0), (0, 0), (0, 0)))
    t = jnp.arange(width)[:, None]
    win = jnp.array(POOL_WINDOWS, jnp.int32)[None, :]
    lo = jnp.clip(t - win // 2, 0, width)
    hi = jnp.clip(t - win // 2 + win, 0, width)
    grp = jnp.arange(N_POOL_GROUPS)[None, :]
    window_sum = cs[:, :, hi, grp] - cs[:, :, lo, grp]
    mean = window_sum / (hi - lo).astype(jnp.float32)[:, :, None]
    pooled = (mean - xg).reshape(B, L, N_POOL_GROUPS, POOL_GC).astype(xb.dtype)
    y = jnp.einsum('blgi,gio->blgo', pooled, w_pool).reshape(B, L, D_MODEL)
    return y * pool_scale


def _fourier_mix(xc, w_fourier):
    B, L, _ = xc.shape
    xg = xc.astype(jnp.float32).reshape(B, L, FOURIER_GROUPS, FOURIER_GC)
    f = jnp.fft.fft2(xg, axes=(1, 3), norm='ortho').real
    return f.reshape(B, L, D_FOURIER).astype(xc.dtype) @ w_fourier


def _mixer(u, rows, h0_f, h0_b, w_in, conv_w, conv_b, lru_wa, lru_ba, lru_wx, lru_bx, lru_lambda,
           w_lru_out, w_pool, pool_scale, w_fourier, b_gate, w_out):
    xa, ya, xb, xc, gl = jnp.split(u @ w_in, [I_YA, I_XB, I_XC, I_G], axis=-1)
    h_f, h_b = _bidir_rglru(xa, conv_w, conv_b, lru_wa, lru_ba, lru_wx, lru_bx, lru_lambda, h0_f, h0_b)
    lru_out = ((h_f + h_b).astype(u.dtype) * jax.nn.gelu(ya)) @ w_lru_out
    pool_out = _multiscale_pool(xb, rows, w_pool, pool_scale)
    four_out = _fourier_mix(xc, w_fourier)
    g_lru, g_pool, g_four = jnp.split(jax.nn.sigmoid(gl + b_gate), N_BRANCHES, axis=-1)
    merged = g_lru * lru_out + g_pool * pool_out + g_four * four_out
    return merged @ w_out, h_f, h_b


def _swiglu(u, w_ffn_in, w_ffn_out):
    gate, up = jnp.split(u @ w_ffn_in, 2, axis=-1)
    return (jax.nn.silu(gate) * up) @ w_ffn_out


def setup_inputs(seed: int = 0) -> dict:
    key = jax.random.key(seed)
    ks = jax.random.split(key, 26)
    f32 = jnp.float32

    def nrm(k, shape, scale):
        return jax.random.normal(k, shape, f32) * scale

    a_init = jax.random.uniform(ks[15], (DEPTH, 2, D_LRU), f32, 0.9, 0.999) ** (1.0 / RGLRU_C)
    return {
        'x': nrm(ks[0], (BATCH, SEQ, D_MODEL), 1.0),
        'c': nrm(ks[1], (BATCH, D_MODEL), 1.0),
        'ctx': nrm(ks[2], (BATCH, CTX_LEN, D_MODEL), 1.0),
        'c_ctx': nrm(ks[3], (D_MODEL,), 1.0),
        'w_ada': nrm(ks[4], (DEPTH, D_MODEL, N_ADA * D_MODEL), 0.5 * D_MODEL ** -0.5),
        'b_ada': nrm(ks[5], (DEPTH, N_ADA * D_MODEL), 0.1),
        'norm1_g': 1.0 + nrm(ks[6], (DEPTH, D_MODEL), 0.05),
        'norm2_g': 1.0 + nrm(ks[7], (DEPTH, D_MODEL), 0.05),
        'w_in': nrm(ks[8], (DEPTH, D_MODEL, D_IN), D_MODEL ** -0.5),
        'conv_w': nrm(ks[9], (DEPTH, CONV_W, D_LRU), CONV_W ** -0.5),
        'conv_b': nrm(ks[10], (DEPTH, D_LRU), 0.02),
        'lru_wa': nrm(ks[11], (DEPTH, 2, LRU_HEADS, LRU_BW, LRU_BW), LRU_BW ** -0.5),
        'lru_ba': nrm(ks[12], (DEPTH, 2, LRU_HEADS, LRU_BW), 0.02),
        'lru_wx': nrm(ks[13], (DEPTH, 2, LRU_HEADS, LRU_BW, LRU_BW), LRU_BW ** -0.5),
        'lru_bx': nrm(ks[14], (DEPTH, 2, LRU_HEADS, LRU_BW), 0.02),
        'lru_lambda': jnp.log(a_init) - jnp.log1p(-a_init),
        'w_lru_out': nrm(ks[16], (DEPTH, D_LRU, D_MODEL), D_LRU ** -0.5),
        'w_pool': nrm(ks[17], (DEPTH, N_POOL_GROUPS, POOL_GC, POOL_OUT_GC), POOL_GC ** -0.5),
        'pool_scale': 1.0 + nrm(ks[18], (DEPTH, D_MODEL), 0.05),
        'w_fourier': nrm(ks[19], (DEPTH, D_FOURIER, D_MODEL), D_FOURIER ** -0.5),
        'b_gate': nrm(ks[20], (DEPTH, N_BRANCHES * D_MODEL), 0.02),
        'w_out': nrm(ks[21], (DEPTH, D_MODEL, D_MODEL), D_MODEL ** -0.5),
        'w_ffn_in': nrm(ks[22], (DEPTH, D_MODEL, 2 * D_FF), D_MODEL ** -0.5),
        'w_ffn_out': nrm(ks[23], (DEPTH, D_FF, D_MODEL), D_FF ** -0.5),
        'norm_f_g': 1.0 + nrm(ks[24], (D_MODEL,), 0.05),
    }


def reference(x, c, ctx, c_ctx, w_ada, b_ada, norm1_g, norm2_g, w_in, conv_w, conv_b,
              lru_wa, lru_ba, lru_wx, lru_bx, lru_lambda, w_lru_out, w_pool, pool_scale,
              w_fourier, b_gate, w_out, w_ffn_in, w_ffn_out, norm_f_g):
    rows = x.shape[1] // GRID_W
    silu_c = jax.nn.silu(c)
    silu_cc = jax.nn.silu(c_ctx)
    h_zero = jnp.zeros((ctx.shape[0], D_LRU), jnp.float32)
    for l in range(DEPTH):
        last = l == DEPTH - 1
        mod = silu_c @ w_ada[l] + b_ada[l]
        mod_ctx = silu_cc @ w_ada[l] + b_ada[l]
        sh1, sc1, g1, sh2, sc2, g2 = jnp.split(mod[:, None, :], N_ADA, axis=-1)
        sh1c, sc1c, g1c, sh2c, sc2c, g2c = jnp.split(mod_ctx, N_ADA, axis=-1)
        lru_p = (conv_w[l], conv_b[l], lru_wa[l], lru_ba[l], lru_wx[l], lru_bx[l], lru_lambda[l])
        u_ctx = _modulate(_rms_norm(ctx, norm1_g[l]), sh1c, sc1c)
        if last:
            hc_f, hc_b = _bidir_rglru(u_ctx @ w_in[l][:, :D_LRU], *lru_p, h_zero, h_zero)
        else:
            mix_ctx, hc_f, hc_b = _mixer(u_ctx, 1, h_zero, h_zero, w_in[l], *lru_p, w_lru_out[l],
                                         w_pool[l], pool_scale[l], w_fourier[l], b_gate[l], w_out[l])
        h0_f = hc_f[:, -1]
        h0_b = hc_b[:, 0]
        u = _modulate(_rms_norm(x, norm1_g[l]), sh1, sc1)
        mix, _, _ = _mixer(u, rows, h0_f, h0_b, w_in[l], *lru_p, w_lru_out[l],
                           w_pool[l], pool_scale[l], w_fourier[l], b_gate[l], w_out[l])
        x = x + g1 * mix
        x = x + g2 * _swiglu(_modulate(_rms_norm(x, norm2_g[l]), sh2, sc2), w_ffn_in[l], w_ffn_out[l])
        if not last:
            ctx = ctx + g1c * mix_ctx
            ctx = ctx + g2c * _swiglu(_modulate(_rms_norm(ctx, norm2_g[l]), sh2c, sc2c),
                                      w_ffn_in[l], w_ffn_out[l])
    return _rms_norm(x, norm_f_g)
```

```python
import functools
import math

import numpy as np
import jax
import jax.numpy as jnp
from jax import lax
from jax.experimental import pallas as pl
from jax.experimental.pallas import tpu as pltpu

F32 = jnp.float32
BF16 = jnp.bfloat16
HIGHEST = lax.Precision.HIGHEST

D_MODEL = 2048
GRID_W = 64
NORM_EPS = 1e-6
N_ADA = 6
D_LRU = D_MODEL // 2
LRU_HEADS = D_LRU // 128
LRU_BW = 128
CONV_W = 4
CONV_LEFT = 2
RGLRU_C = 8.0
D_POOL = D_MODEL // 4
POOL_WINDOWS = (2, 4, 8, 16)
N_POOL_GROUPS = len(POOL_WINDOWS)
POOL_GC = D_POOL // N_POOL_GROUPS
POOL_OUT_GC = D_MODEL // N_POOL_GROUPS
D_FOURIER = D_MODEL // 4
FOURIER_GROUPS = 4
FOURIER_GC = D_FOURIER // FOURIER_GROUPS
D_PLAIN = 2 * D_LRU + D_POOL + D_FOURIER
D_GATES = 3 * D_MODEL

SUBLANES = 8
LANES = 128
MIB = 1024 * 1024

PROJ_TN = 512
ADA_TN = 1024
MERGE_TM = 256
FFN_TF = 512
DFT1_COLS = 8192
GELU_C = math.sqrt(2.0 / math.pi)


def _dot(a, b):
    return jnp.dot(a, b, preferred_element_type=F32)


def _dot_hi(a, b):
    return jnp.dot(a, b, preferred_element_type=F32, precision=HIGHEST)


def _cparams(semantics, vmem_mib):
    return pltpu.CompilerParams(dimension_semantics=semantics, vmem_limit_bytes=vmem_mib * MIB)


def _norm_mod(x, g, shift, scale):
    ms = jnp.mean(x * x, axis=-1, keepdims=True)
    y = x * lax.rsqrt(ms + NORM_EPS) * g
    return y * (1.0 + scale) + shift


def _ada_kernel(s_ref, w_ref, b_ref, o_ref):
    s = s_ref[...]
    a = (s * jax.nn.sigmoid(s)).astype(BF16)
    o_ref[0] = _dot(a, w_ref[0].astype(BF16)) + b_ref[0]


def _ada(cond_rows, w_ada, b_ada):
    depth, d, n = w_ada.shape
    return pl.pallas_call(
        _ada_kernel,
        out_shape=jax.ShapeDtypeStruct((depth, SUBLANES, n), F32),
        grid=(depth, n // ADA_TN),
        in_specs=[
            pl.BlockSpec((SUBLANES, d), lambda l, j: (0, 0)),
            pl.BlockSpec((1, d, ADA_TN), lambda l, j: (l, 0, j)),
            pl.BlockSpec((1, 1, ADA_TN), lambda l, j: (l, 0, j)),
        ],
        out_specs=pl.BlockSpec((1, SUBLANES, ADA_TN), lambda l, j: (l, 0, j)),
        compiler_params=_cparams(("arbitrary", "arbitrary"), 40),
        name="ada",
    )(cond_rows, w_ada, b_ada.reshape(depth, 1, n))


def _proj_kernel(x_ref, g_ref, sh_ref, sc_ref, w_ref, bg_ref, p_ref, gate_ref, u_ref, *, n_plain):
    j = pl.program_id(1)

    @pl.when(j == 0)
    def _():
        u_ref[...] = _norm_mod(x_ref[...], g_ref[...], sh_ref[...], sc_ref[...]).astype(BF16)

    acc = _dot(u_ref[...], w_ref[...])

    @pl.when(j < n_plain)
    def _():
        p_ref[...] = acc

    @pl.when(j >= n_plain)
    def _():
        gate_ref[...] = jax.nn.sigmoid(acc + bg_ref[...]).astype(BF16)


def _proj_plain_kernel(x_ref, g_ref, sh_ref, sc_ref, w_ref, p_ref, u_ref):
    @pl.when(pl.program_id(1) == 0)
    def _():
        u_ref[...] = _norm_mod(x_ref[...], g_ref[...], sh_ref[...], sc_ref[...]).astype(BF16)

    p_ref[...] = _dot(u_ref[...], w_ref[...])


def _proj(x, g, shift, scale, w, b_gate, tm):
    L, d = x.shape
    n_plain = D_PLAIN // PROJ_TN
    n_gate = D_GATES // PROJ_TN
    vec = pl.BlockSpec((1, d), lambda i, j: (0, 0))
    return pl.pallas_call(
        functools.partial(_proj_kernel, n_plain=n_plain),
        out_shape=(jax.ShapeDtypeStruct((L, D_PLAIN), F32),
                   jax.ShapeDtypeStruct((L, D_GATES), BF16)),
        grid=(L // tm, n_plain + n_gate),
        in_specs=[
            pl.BlockSpec((tm, d), lambda i, j: (i, 0)),
            vec, vec, vec,
            pl.BlockSpec((d, PROJ_TN), lambda i, j: (0, j)),
            pl.BlockSpec((1, PROJ_TN), lambda i, j: (0, jnp.maximum(j - n_plain, 0))),
        ],
        out_specs=(
            pl.BlockSpec((tm, PROJ_TN), lambda i, j: (i, jnp.minimum(j, n_plain - 1))),
            pl.BlockSpec((tm, PROJ_TN), lambda i, j: (i, jnp.maximum(j - n_plain, 0))),
        ),
        scratch_shapes=[pltpu.VMEM((tm, d), BF16)],
        compiler_params=_cparams(("arbitrary", "arbitrary"), 48),
        name="proj",
    )(x, g, shift, scale, w, b_gate)


def _proj_plain(x, g, shift, scale, w, tm):
    L, d = x.shape
    n = w.shape[1]
    vec = pl.BlockSpec((1, d), lambda i, j: (0, 0))
    return pl.pallas_call(
        _proj_plain_kernel,
        out_shape=jax.ShapeDtypeStruct((L, n), F32),
        grid=(L // tm, n // PROJ_TN),
        in_specs=[
            pl.BlockSpec((tm, d), lambda i, j: (i, 0)),
            vec, vec, vec,
            pl.BlockSpec((d, PROJ_TN), lambda i, j: (0, j)),
        ],
        out_specs=pl.BlockSpec((tm, PROJ_TN), lambda i, j: (i, j)),
        scratch_shapes=[pltpu.VMEM((tm, d), BF16)],
        compiler_params=_cparams(("arbitrary", "arbitrary"), 48),
        name="proj_plain",
    )(x, g, shift, scale, w)


def _scan_kernel(xm_ref, xp_ref, xn_ref, cw_ref, cb_ref, wg_ref, ba_ref, bx_ref, lam_ref, h0_ref,
                 o_ref, xe_s, a_s, b_s, h_s, *, reverse, n_tiles, tm):
    i = pl.program_id(0)
    tile = (n_tiles - 1 - i) if reverse else i

    @pl.when(i == 0)
    def _():
        h_s[...] = h0_ref[...]

    halo = SUBLANES
    xe_s[0:halo, :] = jnp.where(tile == 0, 0.0, xp_ref[...])
    xe_s[halo:halo + tm, :] = xm_ref[...]
    xe_s[halo + tm:, :] = jnp.where(tile == n_tiles - 1, 0.0, xn_ref[...])
    xc = cb_ref[...]
    for k in range(CONV_W):
        off = halo + k - CONV_LEFT
        xc = xc + cw_ref[k:k + 1, :] * xe_s[off:off + tm, :]

    r_parts, i_parts = [], []
    for h in range(LRU_HEADS):
        xh = xc[:, h * LRU_BW:(h + 1) * LRU_BW].astype(BF16)
        gh = _dot(xh, wg_ref[h])
        r_parts.append(gh[:, :LRU_BW])
        i_parts.append(gh[:, LRU_BW:])
    r = jax.nn.sigmoid(jnp.concatenate(r_parts, axis=1) + ba_ref[...])
    ig = jax.nn.sigmoid(jnp.concatenate(i_parts, axis=1) + bx_ref[...])
    nl = -lam_ref[...]
    softplus = jnp.maximum(nl, 0.0) + jnp.log1p(jnp.exp(-jnp.abs(nl)))
    log_a = (-RGLRU_C) * r * softplus
    a = jnp.exp(log_a)
    a_s[...] = a
    b_s[...] = jnp.sqrt(1.0 - a * a) * (ig * xc)

    n_groups = tm // SUBLANES

    def group(jg, h):
        base = pl.multiple_of(((n_groups - 1 - jg) if reverse else jg) * SUBLANES, SUBLANES)
        rows = range(SUBLANES - 1, -1, -1) if reverse else range(SUBLANES)
        for rr in rows:
            h = a_s[pl.ds(base + rr, 1), :] * h + b_s[pl.ds(base + rr, 1), :]
            o_ref[pl.ds(base + rr, 1), :] = h
        return h

    h_s[...] = lax.fori_loop(0, n_groups, group, h_s[...])


def _scan(p, conv_w, conv_b, wg, ba, bx, lam, h0, *, reverse, tm):
    L = p.shape[0]
    n_tiles = L // tm
    rpt = tm // SUBLANES
    last_row_blk = L // SUBLANES - 1

    def tile_of(i):
        return (n_tiles - 1 - i) if reverse else i

    vec = pl.BlockSpec((1, D_LRU), lambda i: (0, 0))
    return pl.pallas_call(
        functools.partial(_scan_kernel, reverse=reverse, n_tiles=n_tiles, tm=tm),
        out_shape=jax.ShapeDtypeStruct((L, D_LRU), F32),
        grid=(n_tiles,),
        in_specs=[
            pl.BlockSpec((tm, D_LRU), lambda i: (tile_of(i), 0)),
            pl.BlockSpec((SUBLANES, D_LRU), lambda i: (jnp.maximum(tile_of(i) * rpt - 1, 0), 0)),
            pl.BlockSpec((SUBLANES, D_LRU),
                         lambda i: (jnp.minimum((tile_of(i) + 1) * rpt, last_row_blk), 0)),
            pl.BlockSpec((CONV_W, D_LRU), lambda i: (0, 0)),
            vec,
            pl.BlockSpec((LRU_HEADS, LRU_BW, 2 * LRU_BW), lambda i: (0, 0, 0)),
            vec, vec, vec, vec,
        ],
        out_specs=pl.BlockSpec((tm, D_LRU), lambda i: (tile_of(i), 0)),
        scratch_shapes=[
            pltpu.VMEM((tm + 2 * SUBLANES, D_LRU), F32),
            pltpu.VMEM((tm, D_LRU), F32),
            pltpu.VMEM((tm, D_LRU), F32),
            pltpu.VMEM((1, D_LRU), F32),
        ],
        compiler_params=_cparams(("arbitrary",), 32),
        name="scan_bwd" if reverse else "scan_fwd",
    )(p, p, p, conv_w, conv_b, wg, ba, bx, lam, h0)


def _dft1_kernel(x_ref, wr_ref, wi_ref, yr_ref, yi_ref):
    x = x_ref[...]
    yr_ref[...] = _dot_hi(wr_ref[...], x)
    yi_ref[...] = _dot_hi(wi_ref[...], x)


def _dft2_kernel(yr_ref, yi_ref, mr_ref, mi_ref, cc_ref, cs_ref, o_ref, *, scale):
    yr, yi, mr, mi = yr_ref[0], yi_ref[0], mr_ref[0], mi_ref[0]
    zr = _dot_hi(mr, yr) - _dot_hi(mi, yi)
    zi = _dot_hi(mr, yi) + _dot_hi(mi, yr)
    cc, cs = cc_ref[...], cs_ref[...]
    for g in range(FOURIER_GROUPS):
        sl = slice(g * FOURIER_GC, (g + 1) * FOURIER_GC)
        o_ref[:, sl] = (_dot_hi(zr[:, sl], cc) + _dot_hi(zi[:, sl], cs)) * scale


def _dft_tables(n1, n2):
    L = n1 * n2
    k1 = jnp.arange(n1, dtype=jnp.int32)
    m1 = (k1[:, None] * k1[None, :]) % n1
    ang1 = m1.astype(F32) * (2.0 * math.pi / n1)
    k2 = jnp.arange(n2, dtype=jnp.int32)
    kk = k1[:, None, None] + n1 * k2[None, :, None]
    m2 = (kk * k2[None, None, :]) % L
    ang2 = m2.astype(F32) * (2.0 * math.pi / L)
    c = jnp.arange(FOURIER_GC, dtype=jnp.int32)
    mc = (c[:, None] * c[None, :]) % FOURIER_GC
    angc = mc.astype(F32) * (2.0 * math.pi / FOURIER_GC)
    return (jnp.cos(ang1), -jnp.sin(ang1), jnp.cos(ang2), -jnp.sin(ang2),
            jnp.cos(angc), jnp.sin(angc))


def _fourier(xc, n1, n2, tables):
    L = n1 * n2
    w1r, w1i, mr, mi, cc, cs = tables
    ncols = n2 * D_FOURIER
    cb = min(DFT1_COLS, ncols)
    x1 = xc.reshape(n1, ncols)
    yr, yi = pl.pallas_call(
        _dft1_kernel,
        out_shape=(jax.ShapeDtypeStruct((n1, ncols), F32),) * 2,
        grid=(ncols // cb,),
        in_specs=[
            pl.BlockSpec((n1, cb), lambda j: (0, j)),
            pl.BlockSpec((n1, n1), lambda j: (0, 0)),
            pl.BlockSpec((n1, n1), lambda j: (0, 0)),
        ],
        out_specs=(pl.BlockSpec((n1, cb), lambda j: (0, j)),) * 2,
        compiler_params=_cparams(("arbitrary",), 48),
        name="dft1",
    )(x1, w1r, w1i)
    slab = pl.BlockSpec((1, n2, D_FOURIER), lambda k: (k, 0, 0))
    mat = pl.BlockSpec((1, n2, n2), lambda k: (k, 0, 0))
    small = pl.BlockSpec((FOURIER_GC, FOURIER_GC), lambda k: (0, 0))
    out = pl.pallas_call(
        functools.partial(_dft2_kernel, scale=1.0 / math.sqrt(L * FOURIER_GC)),
        out_shape=jax.ShapeDtypeStruct((n2, n1 * D_FOURIER), F32),
        grid=(n1,),
        in_specs=[slab, slab, mat, mat, small, small],
        out_specs=pl.BlockSpec((n2, D_FOURIER), lambda k: (0, k)),
        compiler_params=_cparams(("arbitrary",), 32),
        name="dft2",
    )(yr.reshape(n1, n2, D_FOURIER), yi.reshape(n1, n2, D_FOURIER), mr, mi, cc, cs)
    return out.reshape(L, D_FOURIER)


def _pool_tables(width, tm):
    t = np.arange(tm)
    pos, row = t % width, t // width
    mats, inv = [], []
    for w in POOL_WINDOWS:
        lo = np.clip(pos - w // 2, 0, width)
        hi = np.clip(pos - w // 2 + w, 0, width)
        m = (row[:, None] == row[None, :]) & (pos[None, :] >= lo[:, None]) & (pos[None, :] < hi[:, None])
        mats.append(m.astype(np.float32))
        inv.append(np.repeat((1.0 / (hi - lo))[:, None], POOL_GC, axis=1))
    return (jnp.asarray(np.stack(mats), dtype=BF16),
            jnp.asarray(np.concatenate(inv, axis=1), dtype=F32))


def _merge_kernel(hf_ref, hb_ref, ya_ref, xb_ref, fo_ref, gt_ref, x_ref, g1_ref,
                  wl_ref, pm_ref, ic_ref, wp_ref, ps_ref, wf_ref, wo_ref, o_ref):
    ya = ya_ref[...]
    gelu = 0.5 * ya * (1.0 + jnp.tanh(GELU_C * (ya + 0.044715 * (ya * ya * ya))))
    z = ((hf_ref[...] + hb_ref[...]) * gelu).astype(BF16)
    lru_out = _dot(z, wl_ref[...])
    four_out = _dot(fo_ref[...].astype(BF16), wf_ref[...])

    pool_parts = []
    for g in range(N_POOL_GROUPS):
        xg = xb_ref[:, g * POOL_GC:(g + 1) * POOL_GC]
        hi = xg.astype(BF16)
        lo = (xg - hi.astype(F32)).astype(BF16)
        wsum = _dot(pm_ref[g], hi) + _dot(pm_ref[g], lo)
        pooled = wsum * ic_ref[:, g * POOL_GC:(g + 1) * POOL_GC] - xg
        pool_parts.append(_dot(pooled.astype(BF16), wp_ref[g]))
    pool_out = jnp.concatenate(pool_parts, axis=1) * ps_ref[...]

    d = D_MODEL
    merged = (gt_ref[:, 0:d].astype(F32) * lru_out
              + gt_ref[:, d:2 * d].astype(F32) * pool_out
              + gt_ref[:, 2 * d:3 * d].astype(F32) * four_out)
    y = _dot(merged.astype(BF16), wo_ref[...])
    o_ref[...] = x_ref[...] + g1_ref[...] * y


def _merge(hf, hb, plain, fo, gates, x, g1, w_lru_out, pool_mats, pool_inv, w_pool, pool_scale,
           w_fourier, w_out, tm):
    L, d = x.shape

    def const(shape):
        nd = len(shape)
        return pl.BlockSpec(shape, lambda i: (0,) * nd, pipeline_mode=pl.Buffered(1))

    return pl.pallas_call(
        _merge_kernel,
        out_shape=jax.ShapeDtypeStruct((L, d), F32),
        grid=(L // tm,),
        in_specs=[
            pl.BlockSpec((tm, D_LRU), lambda i: (i, 0)),
            pl.BlockSpec((tm, D_LRU), lambda i: (i, 0)),
            pl.BlockSpec((tm, D_LRU), lambda i: (i, 1)),
            pl.BlockSpec((tm, D_POOL), lambda i: (i, 2 * D_LRU // D_POOL)),
            pl.BlockSpec((tm, D_FOURIER), lambda i: (i, 0)),
            pl.BlockSpec((tm, D_GATES), lambda i: (i, 0)),
            pl.BlockSpec((tm, d), lambda i: (i, 0)),
            const((1, d)),
            const((D_LRU, d)),
            const((N_POOL_GROUPS, tm, tm)),
            const((tm, D_POOL)),
            const((N_POOL_GROUPS, POOL_GC, POOL_OUT_GC)),
            const((1, d)),
            const((D_FOURIER, d)),
            const((d, d)),
        ],
        out_specs=pl.BlockSpec((tm, d), lambda i: (i, 0)),
        compiler_params=_cparams(("arbitrary",), 56),
        name="merge",
    )(hf, hb, plain, plain, fo, gates, x, g1, w_lru_out, pool_mats, pool_inv, w_pool, pool_scale,
      w_fourier, w_out)


def _ffn_kernel(x_ref, g_ref, sh_ref, sc_ref, g2_ref, nf_ref, wg_ref, wu_ref, wo_ref, o_ref,
                u_s, acc_s, *, n_f, final_norm):
    f = pl.program_id(1)

    @pl.when(f == 0)
    def _():
        u_s[...] = _norm_mod(x_ref[...], g_ref[...], sh_ref[...], sc_ref[...]).astype(BF16)
        acc_s[...] = jnp.zeros_like(acc_s)

    u = u_s[...]
    hg = _dot(u, wg_ref[...])
    hu = _dot(u, wu_ref[...])
    act = (hg * jax.nn.sigmoid(hg) * hu).astype(BF16)
    acc_s[...] += _dot(act, wo_ref[...])

    @pl.when(f == n_f - 1)
    def _():
        y = x_ref[...] + g2_ref[...] * acc_s[...]
        if final_norm:
            ms = jnp.mean(y * y, axis=-1, keepdims=True)
            y = y * lax.rsqrt(ms + NORM_EPS) * nf_ref[...]
        o_ref[...] = y


def _ffn(x, g, shift, scale, g2, norm_f, w_in, w_out, tm, final_norm):
    L, d = x.shape
    d_ff = w_out.shape[0]
    n_f = d_ff // FFN_TF
    vec = pl.BlockSpec((1, d), lambda i, f: (0, 0))
    return pl.pallas_call(
        functools.partial(_ffn_kernel, n_f=n_f, final_norm=final_norm),
        out_shape=jax.ShapeDtypeStruct((L, d), F32),
        grid=(L // tm, n_f),
        in_specs=[
            pl.BlockSpec((tm, d), lambda i, f: (i, 0)),
            vec, vec, vec, vec, vec,
            pl.BlockSpec((d, FFN_TF), lambda i, f: (0, f)),
            pl.BlockSpec((d, FFN_TF), lambda i, f: (0, f + n_f)),
            pl.BlockSpec((FFN_TF, d), lambda i, f: (f, 0)),
        ],
        out_specs=pl.BlockSpec((tm, d), lambda i, f: (i, 0)),
        scratch_shapes=[pltpu.VMEM((tm, d), BF16), pltpu.VMEM((tm, d), F32)],
        compiler_params=_cparams(("arbitrary", "arbitrary"), 48),
        name="ffn",
    )(x, g, shift, scale, g2, norm_f, w_in, w_in, w_out)


def _dft_factors(L):
    n1 = 1 << ((L.bit_length() - 1) // 2)
    return n1, L // n1


def kernel(x, c, ctx, c_ctx, w_ada, b_ada, norm1_g, norm2_g, w_in, conv_w, conv_b, lru_wa, lru_ba,
           lru_wx, lru_bx, lru_lambda, w_lru_out, w_pool, pool_scale, w_fourier, b_gate, w_out,
           w_ffn_in, w_ffn_out, norm_f_g):
    depth = w_ada.shape[0]
    batch, seq, d = x.shape
    ctx_len = ctx.shape[1]
    assert batch == 1 and d == D_MODEL
    xs = x[0]
    cs = ctx[0]

    cond_rows = jnp.concatenate(
        [c, c_ctx[None, :], jnp.zeros((SUBLANES - 2, d), F32)], axis=0)
    mod = _ada(cond_rows, w_ada, b_ada)

    x_tm = 1024
    x_f = _dft_factors(seq)
    c_f = _dft_factors(ctx_len)
    x_tabs = _dft_tables(*x_f)
    c_tabs = _dft_tables(*c_f)
    x_pool = _pool_tables(GRID_W, MERGE_TM)
    c_pool = _pool_tables(ctx_len, MERGE_TM)
    h_zero = jnp.zeros((1, D_LRU), F32)
    row = lambda v: v.reshape(1, -1)

    for l in range(depth):
        last = l == depth - 1
        m_x = [mod[l, 0:1, k * d:(k + 1) * d] for k in range(N_ADA)]
        m_c = [mod[l, 1:2, k * d:(k + 1) * d] for k in range(N_ADA)]
        w_in_b = w_in[l].astype(BF16)
        wg = [jnp.concatenate([lru_wa[l, s], lru_wx[l, s]], axis=-1).astype(BF16) for s in range(2)]
        scan_args = [
            (conv_w[l], row(conv_b[l]), wg[s], row(lru_ba[l, s]), row(lru_bx[l, s]),
             row(lru_lambda[l, s])) for s in range(2)]
        n1g, bg = row(norm1_g[l]), row(b_gate[l])
        mix_w = (w_lru_out[l].astype(BF16),)
        mix_w2 = (w_pool[l].astype(BF16), row(pool_scale[l]), w_fourier[l].astype(BF16),
                  w_out[l].astype(BF16))
        ffn_w = (w_ffn_in[l].astype(BF16), w_ffn_out[l].astype(BF16))

        if last:
            plain_c = _proj_plain(cs, n1g, m_c[0], m_c[1], w_in_b[:, :D_LRU], ctx_len)
        else:
            plain_c, gates_c = _proj(cs, n1g, m_c[0], m_c[1], w_in_b, bg, ctx_len)
        hf_c = _scan(plain_c, *scan_args[0], h_zero, reverse=False, tm=ctx_len)
        hb_c = _scan(plain_c, *scan_args[1], h_zero, reverse=True, tm=ctx_len)
        h0_f = hf_c[ctx_len - 1:ctx_len]
        h0_b = hb_c[0:1]
        if not last:
            fo_c = _fourier(plain_c[:, D_PLAIN - D_FOURIER:], *c_f, c_tabs)
            cs = _merge(hf_c, hb_c, plain_c, fo_c, gates_c, cs, m_c[2], *mix_w, *c_pool, *mix_w2,
                        MERGE_TM)
            cs = _ffn(cs, row(norm2_g[l]), m_c[3], m_c[4], m_c[5], row(norm_f_g), *ffn_w,
                      ctx_len, False)

        plain, gates = _proj(xs, n1g, m_x[0], m_x[1], w_in_b, bg, x_tm)
        hf = _scan(plain, *scan_args[0], h0_f, reverse=False, tm=MERGE_TM)
        hb = _scan(plain, *scan_args[1], h0_b, reverse=True, tm=MERGE_TM)
        fo = _fourier(plain[:, D_PLAIN - D_FOURIER:], *x_f, x_tabs)
        xs = _merge(hf, hb, plain, fo, gates, xs, m_x[2], *mix_w, *x_pool, *mix_w2, MERGE_TM)
        xs = _ffn(xs, row(norm2_g[l]), m_x[3], m_x[4], m_x[5], row(norm_f_g), *ffn_w, 512, last)

    return xs[None]
```

```python
import functools
import math

import numpy as np
import jax
import jax.numpy as jnp
from jax import lax
from jax.experimental import pallas as pl
from jax.experimental.pallas import tpu as pltpu

F32 = jnp.float32
BF16 = jnp.bfloat16
HIGHEST = lax.Precision.HIGHEST

D_MODEL = 2048
GRID_W = 64
NORM_EPS = 1e-6
N_ADA = 6
D_LRU = D_MODEL // 2
LRU_HEADS = D_LRU // 128
LRU_BW = 128
CONV_W = 4
CONV_LEFT = 2
RGLRU_C = 8.0
D_POOL = D_MODEL // 4
POOL_WINDOWS = (2, 4, 8, 16)
N_POOL_GROUPS = len(POOL_WINDOWS)
POOL_GC = D_POOL // N_POOL_GROUPS
POOL_OUT_GC = D_MODEL // N_POOL_GROUPS
D_FOURIER = D_MODEL // 4
FOURIER_GROUPS = 4
FOURIER_GC = D_FOURIER // FOURIER_GROUPS
D_PLAIN = 2 * D_LRU + D_POOL + D_FOURIER
D_GATES = 3 * D_MODEL

SUBLANES = 8
LANES = 128
MIB = 1024 * 1024

PROJ_TN = 1024
ADA_TN = 1024
MERGE_TM = 256
FFN_TF = 512
DFT_TB = SUBLANES
GELU_C = math.sqrt(2.0 / math.pi)


def _dot(a, b):
    return jnp.dot(a, b, preferred_element_type=F32)


def _dot_hi(a, b):
    return jnp.dot(a, b, preferred_element_type=F32, precision=HIGHEST)


def _cparams(semantics, vmem_mib):
    return pltpu.CompilerParams(dimension_semantics=semantics, vmem_limit_bytes=vmem_mib * MIB)


def _norm_mod(x, g, shift, scale):
    ms = jnp.mean(x * x, axis=-1, keepdims=True)
    y = x * lax.rsqrt(ms + NORM_EPS) * g
    return y * (1.0 + scale) + shift


def _ada_kernel(s_ref, w_ref, b_ref, o_ref):
    s = s_ref[...]
    a = (s * jax.nn.sigmoid(s)).astype(BF16)
    o_ref[0] = _dot(a, w_ref[0].astype(BF16)) + b_ref[0]


def _ada(cond_rows, w_ada, b_ada):
    depth, d, n = w_ada.shape
    return pl.pallas_call(
        _ada_kernel,
        out_shape=jax.ShapeDtypeStruct((depth, SUBLANES, n), F32),
        grid=(depth, n // ADA_TN),
        in_specs=[
            pl.BlockSpec((SUBLANES, d), lambda l, j: (0, 0)),
            pl.BlockSpec((1, d, ADA_TN), lambda l, j: (l, 0, j)),
            pl.BlockSpec((1, 1, ADA_TN), lambda l, j: (l, 0, j)),
        ],
        out_specs=pl.BlockSpec((1, SUBLANES, ADA_TN), lambda l, j: (l, 0, j)),
        compiler_params=_cparams(("arbitrary", "arbitrary"), 40),
        name="ada",
    )(cond_rows, w_ada, b_ada.reshape(depth, 1, n))


def _proj_a_kernel(x_ref, g_ref, sh_ref, sc_ref, w_ref, p_ref, u_ref):
    @pl.when(pl.program_id(1) == 0)
    def _():
        u_ref[...] = _norm_mod(x_ref[...], g_ref[...], sh_ref[...], sc_ref[...]).astype(BF16)

    p_ref[...] = _dot(u_ref[...], w_ref[...])


def _proj_a(x, g, shift, scale, w, tm):
    L, d = x.shape
    n = w.shape[1]
    vec = pl.BlockSpec((1, d), lambda i, j: (0, 0))
    return pl.pallas_call(
        _proj_a_kernel,
        out_shape=(jax.ShapeDtypeStruct((L, n), F32), jax.ShapeDtypeStruct((L, d), BF16)),
        grid=(L // tm, n // PROJ_TN),
        in_specs=[
            pl.BlockSpec((tm, d), lambda i, j: (i, 0)),
            vec, vec, vec,
            pl.BlockSpec((d, PROJ_TN), lambda i, j: (0, j)),
        ],
        out_specs=(pl.BlockSpec((tm, PROJ_TN), lambda i, j: (i, j)),
                   pl.BlockSpec((tm, d), lambda i, j: (i, 0))),
        compiler_params=_cparams(("arbitrary", "arbitrary"), 52),
        name="proj_a",
    )(x, g, shift, scale, w)


def _proj_b_kernel(u_ref, w_ref, b_ref, o_ref):
    o_ref[...] = jax.nn.sigmoid(_dot(u_ref[...], w_ref[...]) + b_ref[...]).astype(BF16)


def _proj_b(u, w, b, tm):
    L, d = u.shape
    n = w.shape[1]
    return pl.pallas_call(
        _proj_b_kernel,
        out_shape=jax.ShapeDtypeStruct((L, n), BF16),
        grid=(L // tm, n // PROJ_TN),
        in_specs=[
            pl.BlockSpec((tm, d), lambda i, j: (i, 0)),
            pl.BlockSpec((d, PROJ_TN), lambda i, j: (0, j)),
            pl.BlockSpec((1, PROJ_TN), lambda i, j: (0, j)),
        ],
        out_specs=pl.BlockSpec((tm, PROJ_TN), lambda i, j: (i, j)),
        compiler_params=_cparams(("arbitrary", "arbitrary"), 40),
        name="proj_b",
    )(u, w, b)


def _scan_kernel(xm_ref, xp_ref, xn_ref, cw_ref, cb_ref, wg_ref, ba_ref, bx_ref, lam_ref, h0_ref,
                 o_ref, xe_s, a_s, b_s, h_s, *, reverse, n_tiles, tm):
    i = pl.program_id(0)
    tile = (n_tiles - 1 - i) if reverse else i

    @pl.when(i == 0)
    def _():
        h_s[...] = h0_ref[...]

    halo = SUBLANES
    xe_s[0:halo, :] = jnp.where(tile == 0, 0.0, xp_ref[...])
    xe_s[halo:halo + tm, :] = xm_ref[...]
    xe_s[halo + tm:, :] = jnp.where(tile == n_tiles - 1, 0.0, xn_ref[...])
    xc = cb_ref[...]
    for k in range(CONV_W):
        off = halo + k - CONV_LEFT
        xc = xc + cw_ref[k:k + 1, :] * xe_s[off:off + tm, :]

    r_parts, i_parts = [], []
    for h in range(LRU_HEADS):
        xh = xc[:, h * LRU_BW:(h + 1) * LRU_BW].astype(BF16)
        gh = _dot(xh, wg_ref[h])
        r_parts.append(gh[:, :LRU_BW])
        i_parts.append(gh[:, LRU_BW:])
    r = jax.nn.sigmoid(jnp.concatenate(r_parts, axis=1) + ba_ref[...])
    ig = jax.nn.sigmoid(jnp.concatenate(i_parts, axis=1) + bx_ref[...])
    nl = -lam_ref[...]
    softplus = jnp.maximum(nl, 0.0) + jnp.log1p(jnp.exp(-jnp.abs(nl)))
    log_a = (-RGLRU_C) * r * softplus
    a = jnp.exp(log_a)
    a_s[...] = a
    b_s[...] = jnp.sqrt(1.0 - a * a) * (ig * xc)

    n_groups = tm // SUBLANES

    def group(jg, h):
        base = pl.multiple_of(((n_groups - 1 - jg) if reverse else jg) * SUBLANES, SUBLANES)
        rows = range(SUBLANES - 1, -1, -1) if reverse else range(SUBLANES)
        for rr in rows:
            h = a_s[pl.ds(base + rr, 1), :] * h + b_s[pl.ds(base + rr, 1), :]
            o_ref[pl.ds(base + rr, 1), :] = h
        return h

    h_s[...] = lax.fori_loop(0, n_groups, group, h_s[...])


def _scan(p, conv_w, conv_b, wg, ba, bx, lam, h0, *, reverse, tm):
    L = p.shape[0]
    n_tiles = L // tm
    rpt = tm // SUBLANES
    last_row_blk = L // SUBLANES - 1

    def tile_of(i):
        return (n_tiles - 1 - i) if reverse else i

    vec = pl.BlockSpec((1, D_LRU), lambda i: (0, 0))
    return pl.pallas_call(
        functools.partial(_scan_kernel, reverse=reverse, n_tiles=n_tiles, tm=tm),
        out_shape=jax.ShapeDtypeStruct((L, D_LRU), F32),
        grid=(n_tiles,),
        in_specs=[
            pl.BlockSpec((tm, D_LRU), lambda i: (tile_of(i), 0)),
            pl.BlockSpec((SUBLANES, D_LRU), lambda i: (jnp.maximum(tile_of(i) * rpt - 1, 0), 0)),
            pl.BlockSpec((SUBLANES, D_LRU),
                         lambda i: (jnp.minimum((tile_of(i) + 1) * rpt, last_row_blk), 0)),
            pl.BlockSpec((CONV_W, D_LRU), lambda i: (0, 0)),
            vec,
            pl.BlockSpec((LRU_HEADS, LRU_BW, 2 * LRU_BW), lambda i: (0, 0, 0)),
            vec, vec, vec, vec,
        ],
        out_specs=pl.BlockSpec((tm, D_LRU), lambda i: (tile_of(i), 0)),
        scratch_shapes=[
            pltpu.VMEM((tm + 2 * SUBLANES, D_LRU), F32),
            pltpu.VMEM((tm, D_LRU), F32),
            pltpu.VMEM((tm, D_LRU), F32),
            pltpu.VMEM((1, D_LRU), F32),
        ],
        compiler_params=_cparams(("arbitrary",), 32),
        name="scan_bwd" if reverse else "scan_fwd",
    )(p, p, p, conv_w, conv_b, wg, ba, bx, lam, h0)


def _wfold_kernel(cc_ref, cs_ref, w_ref, o_ref):
    w = w_ref[...]
    o_ref[0] = _dot_hi(cc_ref[...], w).astype(BF16)
    o_ref[1] = _dot_hi(cs_ref[...], w).astype(BF16)


def _wfold(w_fourier):
    d = w_fourier.shape[1]
    c = np.arange(FOURIER_GC)
    ang = 2.0 * np.pi * ((c[:, None] * c[None, :]) % FOURIER_GC) / FOURIER_GC
    small = pl.BlockSpec((FOURIER_GC, FOURIER_GC), lambda g: (0, 0))
    out = pl.pallas_call(
        _wfold_kernel,
        out_shape=jax.ShapeDtypeStruct((2, D_FOURIER, d), BF16),
        grid=(FOURIER_GROUPS,),
        in_specs=[small, small, pl.BlockSpec((FOURIER_GC, d), lambda g: (g, 0))],
        out_specs=pl.BlockSpec((2, FOURIER_GC, d), lambda g: (0, g, 0)),
        compiler_params=_cparams(("arbitrary",), 32),
        name="wfold",
    )(jnp.asarray(np.cos(ang), F32), jnp.asarray(np.sin(ang), F32), w_fourier)
    return out.reshape(2 * D_FOURIER, d)


def _dft1_kernel(x_ref, w_ref, tr_ref, ti_ref, o_ref, *, n1):
    w = w_ref[...]
    for j in range(DFT_TB):
        y = _dot(w, x_ref[:, j, :])
        yr, yi = y[:n1], y[n1:]
        tr, ti = tr_ref[0][:, j:j + 1], ti_ref[0][:, j:j + 1]
        o_ref[0:n1, j, :] = yr * tr - yi * ti
        o_ref[n1:2 * n1, j, :] = yr * ti + yi * tr


def _dft2_kernel(yr_ref, yi_ref, w_ref, o_ref, *, n2):
    w = w_ref[...]
    for s in range(DFT_TB):
        y = jnp.concatenate([yr_ref[s], yi_ref[s]], axis=0)
        z = _dot(w, y)
        o_ref[:, s, 0:D_FOURIER] = z[:n2]
        o_ref[:, s, D_FOURIER:2 * D_FOURIER] = z[n2:]


def _dft_tables(n1, n2):
    L = n1 * n2
    k1 = np.arange(n1)
    a1 = 2.0 * np.pi * ((k1[:, None] * k1[None, :]) % n1) / n1
    w1 = np.concatenate([np.cos(a1), -np.sin(a1)], axis=0)
    t2 = np.arange(n2)
    ph = 2.0 * np.pi * ((k1[:, None] * t2[None, :]) % L) / L
    def blocked(v):
        out = np.zeros((n2 // DFT_TB, n1, LANES), np.float32)
        out[:, :, :DFT_TB] = v.reshape(n1, n2 // DFT_TB, DFT_TB).transpose(1, 0, 2)
        return jnp.asarray(out)
    a2 = 2.0 * np.pi * ((t2[:, None] * t2[None, :]) % n2) / n2
    c2, s2 = np.cos(a2), np.sin(a2)
    w2 = np.block([[c2, s2], [-s2, c2]]) / math.sqrt(L * FOURIER_GC)
    return jnp.asarray(w1, F32), blocked(np.cos(ph)), blocked(-np.sin(ph)), jnp.asarray(w2, F32)


def _fourier(plain, n1, n2, tables):
    L = n1 * n2
    w1, twr, twi, w2 = tables
    p3 = plain.reshape(n1, n2, D_PLAIN)
    xc_blk = (D_PLAIN - D_FOURIER) // D_FOURIER
    tw_spec = pl.BlockSpec((1, n1, LANES), lambda t: (t, 0, 0))
    y3 = pl.pallas_call(
        functools.partial(_dft1_kernel, n1=n1),
        out_shape=jax.ShapeDtypeStruct((2 * n1, n2, D_FOURIER), F32),
        grid=(n2 // DFT_TB,),
        in_specs=[
            pl.BlockSpec((n1, DFT_TB, D_FOURIER), lambda t: (0, t, xc_blk)),
            pl.BlockSpec((2 * n1, n1), lambda t: (0, 0)),
            tw_spec, tw_spec,
        ],
        out_specs=pl.BlockSpec((2 * n1, DFT_TB, D_FOURIER), lambda t: (0, t, 0)),
        compiler_params=_cparams(("arbitrary",), 32),
        name="dft1",
    )(p3, w1, twr, twi)
    n_blk = n1 // DFT_TB
    z3 = pl.pallas_call(
        functools.partial(_dft2_kernel, n2=n2),
        out_shape=jax.ShapeDtypeStruct((n2, n1, 2 * D_FOURIER), F32),
        grid=(n_blk,),
        in_specs=[
            pl.BlockSpec((DFT_TB, n2, D_FOURIER), lambda k: (k, 0, 0)),
            pl.BlockSpec((DFT_TB, n2, D_FOURIER), lambda k: (n_blk + k, 0, 0)),
            pl.BlockSpec((2 * n2, 2 * n2), lambda k: (0, 0)),
        ],
        out_specs=pl.BlockSpec((n2, DFT_TB, 2 * D_FOURIER), lambda k: (0, k, 0)),
        compiler_params=_cparams(("arbitrary",), 32),
        name="dft2",
    )(y3, y3, w2)
    return z3.reshape(L, 2 * D_FOURIER)


def _pool_tables(width, tm):
    t = np.arange(tm)
    pos, row = t % width, t // width
    mats, inv = [], []
    for w in POOL_WINDOWS:
        lo = np.clip(pos - w // 2, 0, width)
        hi = np.clip(pos - w // 2 + w, 0, width)
        m = (row[:, None] == row[None, :]) & (pos[None, :] >= lo[:, None]) & (pos[None, :] < hi[:, None])
        mats.append(m.astype(np.float32))
        inv.append(np.repeat((1.0 / (hi - lo))[:, None], POOL_GC, axis=1))
    return (jnp.asarray(np.stack(mats), dtype=BF16),
            jnp.asarray(np.concatenate(inv, axis=1), dtype=F32))


def _merge_kernel(hf_ref, hb_ref, ya_ref, xb_ref, fo_ref, gt_ref, x_ref, g1_ref,
                  wl_ref, pm_ref, ic_ref, wp_ref, ps_ref, wf_ref, wo_ref, o_ref):
    ya = ya_ref[...]
    gelu = 0.5 * ya * (1.0 + jnp.tanh(GELU_C * (ya + 0.044715 * (ya * ya * ya))))
    z = ((hf_ref[...] + hb_ref[...]) * gelu).astype(BF16)
    lru_out = _dot(z, wl_ref[...])
    four_out = _dot(fo_ref[...].astype(BF16), wf_ref[...])

    pool_parts = []
    for g in range(N_POOL_GROUPS):
        xg = xb_ref[:, g * POOL_GC:(g + 1) * POOL_GC]
        hi = xg.astype(BF16)
        lo = (xg - hi.astype(F32)).astype(BF16)
        wsum = _dot(pm_ref[g], hi) + _dot(pm_ref[g], lo)
        pooled = wsum * ic_ref[:, g * POOL_GC:(g + 1) * POOL_GC] - xg
        pool_parts.append(_dot(pooled.astype(BF16), wp_ref[g]))
    pool_out = jnp.concatenate(pool_parts, axis=1) * ps_ref[...]

    d = D_MODEL
    merged = (gt_ref[:, 0:d].astype(F32) * lru_out
              + gt_ref[:, d:2 * d].astype(F32) * pool_out
              + gt_ref[:, 2 * d:3 * d].astype(F32) * four_out)
    y = _dot(merged.astype(BF16), wo_ref[...])
    o_ref[...] = x_ref[...] + g1_ref[...] * y


def _merge(hf, hb, plain, fo, gates, x, g1, w_lru_out, pool_mats, pool_inv, w_pool, pool_scale,
           w_fourier, w_out, tm):
    L, d = x.shape

    def const(shape):
        nd = len(shape)
        return pl.BlockSpec(shape, lambda i: (0,) * nd, pipeline_mode=pl.Buffered(1))

    return pl.pallas_call(
        _merge_kernel,
        out_shape=jax.ShapeDtypeStruct((L, d), F32),
        grid=(L // tm,),
        in_specs=[
            pl.BlockSpec((tm, D_LRU), lambda i: (i, 0)),
            pl.BlockSpec((tm, D_LRU), lambda i: (i, 0)),
            pl.BlockSpec((tm, D_LRU), lambda i: (i, 1)),
            pl.BlockSpec((tm, D_POOL), lambda i: (i, 2 * D_LRU // D_POOL)),
            pl.BlockSpec((tm, 2 * D_FOURIER), lambda i: (i, 0)),
            pl.BlockSpec((tm, D_GATES), lambda i: (i, 0)),
            pl.BlockSpec((tm, d), lambda i: (i, 0)),
            const((1, d)),
            const((D_LRU, d)),
            const((N_POOL_GROUPS, tm, tm)),
            const((tm, D_POOL)),
            const((N_POOL_GROUPS, POOL_GC, POOL_OUT_GC)),
            const((1, d)),
            const((2 * D_FOURIER, d)),
            const((d, d)),
        ],
        out_specs=pl.BlockSpec((tm, d), lambda i: (i, 0)),
        compiler_params=_cparams(("arbitrary",), 56),
        name="merge",
    )(hf, hb, plain, plain, fo, gates, x, g1, w_lru_out, pool_mats, pool_inv, w_pool, pool_scale,
      w_fourier, w_out)


def _ffn_kernel(x_ref, g_ref, sh_ref, sc_ref, g2_ref, nf_ref, wg_ref, wu_ref, wo_ref, o_ref,
                u_s, acc_s, *, n_f, final_norm):
    f = pl.program_id(1)

    @pl.when(f == 0)
    def _():
        u_s[...] = _norm_mod(x_ref[...], g_ref[...], sh_ref[...], sc_ref[...]).astype(BF16)
        acc_s[...] = jnp.zeros_like(acc_s)

    u = u_s[...]
    hg = _dot(u, wg_ref[...])
    hu = _dot(u, wu_ref[...])
    act = (hg * jax.nn.sigmoid(hg) * hu).astype(BF16)
    acc_s[...] += _dot(act, wo_ref[...])

    @pl.when(f == n_f - 1)
    def _():
        y = x_ref[...] + g2_ref[...] * acc_s[...]
        if final_norm:
            ms = jnp.mean(y * y, axis=-1, keepdims=True)
            y = y * lax.rsqrt(ms + NORM_EPS) * nf_ref[...]
        o_ref[...] = y


def _ffn(x, g, shift, scale, g2, norm_f, w_in, w_out, tm, final_norm):
    L, d = x.shape
    d_ff = w_out.shape[0]
    n_f = d_ff // FFN_TF
    vec = pl.BlockSpec((1, d), lambda i, f: (0, 0))
    return pl.pallas_call(
        functools.partial(_ffn_kernel, n_f=n_f, final_norm=final_norm),
        out_shape=jax.ShapeDtypeStruct((L, d), F32),
        grid=(L // tm, n_f),
        in_specs=[
            pl.BlockSpec((tm, d), lambda i, f: (i, 0)),
            vec, vec, vec, vec, vec,
            pl.BlockSpec((d, FFN_TF), lambda i, f: (0, f)),
            pl.BlockSpec((d, FFN_TF), lambda i, f: (0, f + n_f)),
            pl.BlockSpec((FFN_TF, d), lambda i, f: (f, 0)),
        ],
        out_specs=pl.BlockSpec((tm, d), lambda i, f: (i, 0)),
        scratch_shapes=[pltpu.VMEM((tm, d), BF16), pltpu.VMEM((tm, d), F32)],
        compiler_params=_cparams(("arbitrary", "arbitrary"), 48),
        name="ffn",
    )(x, g, shift, scale, g2, norm_f, w_in, w_in, w_out)


def _dft_factors(L):
    n1 = 1 << ((L.bit_length() - 1) // 2)
    return n1, L // n1


def kernel(x, c, ctx, c_ctx, w_ada, b_ada, norm1_g, norm2_g, w_in, conv_w, conv_b, lru_wa, lru_ba,
           lru_wx, lru_bx, lru_lambda, w_lru_out, w_pool, pool_scale, w_fourier, b_gate, w_out,
           w_ffn_in, w_ffn_out, norm_f_g):
    depth = w_ada.shape[0]
    batch, seq, d = x.shape
    ctx_len = ctx.shape[1]
    assert batch == 1 and d == D_MODEL
    xs = x[0]
    cs = ctx[0]

    cond_rows = jnp.concatenate(
        [c, c_ctx[None, :], jnp.zeros((SUBLANES - 2, d), F32)], axis=0)
    mod = _ada(cond_rows, w_ada, b_ada)

    x_tm = 1024
    x_f = _dft_factors(seq)
    c_f = _dft_factors(ctx_len)
    x_tabs = _dft_tables(*x_f)
    c_tabs = _dft_tables(*c_f)
    x_pool = _pool_tables(GRID_W, MERGE_TM)
    c_pool = _pool_tables(ctx_len, MERGE_TM)
    h_zero = jnp.zeros((1, D_LRU), F32)
    row = lambda v: v.reshape(1, -1)

    for l in range(depth):
        last = l == depth - 1
        m_x = [mod[l, 0:1, k * d:(k + 1) * d] for k in range(N_ADA)]
        m_c = [mod[l, 1:2, k * d:(k + 1) * d] for k in range(N_ADA)]
        w_plain = w_in[l, :, :D_PLAIN].astype(BF16)
        w_gates = w_in[l, :, D_PLAIN:].astype(BF16)
        wg =[jnp.concatenate([lru_wa[l, s], lru_wx[l, s]], axis=-1).astype(BF16) for s in range(2)]
        scan_args = [
            (conv_w[l], row(conv_b[l]), wg[s], row(lru_ba[l, s]), row(lru_bx[l, s]),
             row(lru_lambda[l, s])) for s in range(2)]
        n1g, bg = row(norm1_g[l]), row(b_gate[l])
        mix_w = (w_lru_out[l].astype(BF16),)
        mix_w2 = (w_pool[l].astype(BF16), row(pool_scale[l]), _wfold(w_fourier[l]),
                  w_out[l].astype(BF16))
        ffn_w = (w_ffn_in[l].astype(BF16), w_ffn_out[l].astype(BF16))

        if last:
            plain_c, _ = _proj_a(cs, n1g, m_c[0], m_c[1], w_plain[:, :D_LRU], ctx_len)
        else:
            plain_c, u_c = _proj_a(cs, n1g, m_c[0], m_c[1], w_plain, ctx_len)
            gates_c = _proj_b(u_c, w_gates, bg, ctx_len)
        hf_c = _scan(plain_c, *scan_args[0], h_zero, reverse=False, tm=ctx_len)
        hb_c = _scan(plain_c, *scan_args[1], h_zero, reverse=True, tm=ctx_len)
        h0_f = hf_c[ctx_len - 1:ctx_len]
        h0_b = hb_c[0:1]
        if not last:
            fo_c = _fourier(plain_c, *c_f, c_tabs)
            cs = _merge(hf_c, hb_c, plain_c, fo_c, gates_c, cs, m_c[2], *mix_w, *c_pool, *mix_w2,
                        MERGE_TM)
            cs = _ffn(cs, row(norm2_g[l]), m_c[3], m_c[4], m_c[5], row(norm_f_g), *ffn_w,
                      ctx_len, False)

        plain, u_x = _proj_a(xs, n1g, m_x[0], m_x[1], w_plain, x_tm)
        gates = _proj_b(u_x, w_gates, bg, x_tm)
        hf = _scan(plain, *scan_args[0], h0_f, reverse=False, tm=MERGE_TM)
        hb = _scan(plain, *scan_args[1], h0_b, reverse=True, tm=MERGE_TM)
        fo = _fourier(plain, *x_f, x_tabs)
        xs = _merge(hf, hb, plain, fo, gates, xs, m_x[2], *mix_w, *x_pool, *mix_w2, MERGE_TM)
        xs = _ffn(xs, row(norm2_g[l]), m_x[3], m_x[4], m_x[5], row(norm_f_g), *ffn_w, 512, last)

    return xs[None]
```
